```python
import math
import jax, jax.numpy as jnp
from jax import lax
import numpy as np

D_MODEL = 2048
BATCH = 4
SEQ = 2048
DEPTH = 2

GRID_W = 64
CTX_LEN = 256
HEAD_DIM = 128
N_HEADS = D_MODEL // HEAD_DIM
MLSTM_HEADS = N_HEADS // 2
DIFF_HEADS = N_HEADS - MLSTM_HEADS
MLSTM_W = MLSTM_HEADS * HEAD_DIM
DIFF_W = DIFF_HEADS * HEAD_DIM
DIFF_QK_DIM = HEAD_DIM // 2
MLSTM_CHUNK = 128
MLSTM_CONV = 5
NA_HEADS = N_HEADS
NA_WIN_ROWS = 8
NA_WIN_COLS = 16
D_FF = 5632
ATTN_BLOCK = 128
ROPE_BASE = 10000.0
NORM_EPS = 1e-6
FFN_RES = 0.5
N_MOD = 9
N_EVEN = (DEPTH + 1) // 2
N_ODD = DEPTH // 2
EVEN_SPLIT = (MLSTM_W, MLSTM_W, MLSTM_W, MLSTM_W, 4 * MLSTM_HEADS, DIFF_W, DIFF_W, DIFF_W)
EVEN_IN = sum(EVEN_SPLIT)
ODD_IN = 3 * NA_HEADS * HEAD_DIM

kernel_name = 'hybrid_mlstm_diffattn_natten_macaron'


def rms_norm(x, g):
    xf = x.astype(jnp.float32)
    xf = xf * lax.rsqrt(jnp.mean(xf * xf, axis=-1, keepdims=True) + NORM_EPS)
    return xf.astype(x.dtype) * g


def swiglu(h, w_i, w_o):
    gt, up = jnp.split(h @ w_i, 2, axis=-1)
    return (jax.nn.silu(gt) * up) @ w_o


def ffn_sublayer(h, mod, g_pre, g_post, w_i, w_o):
    shift, scale, gate = mod
    u = rms_norm(h, g_pre) * (1 + scale) + shift
    return h + FFN_RES * gate * rms_norm(swiglu(u, w_i, w_o), g_post)


def split_cols(t, sizes):
    return jnp.split(t, [int(s) for s in np.cumsum(sizes)[:-1]], axis=-1)


def to_heads(t, n_heads):
    B, T, _ = t.shape
    return t.reshape(B, T, n_heads, -1).transpose(0, 2, 1, 3)


def from_heads(t):
    B, H, T, d = t.shape
    return t.transpose(0, 2, 1, 3).reshape(B, T, H * d)


def axial_rope_angles(n_tok, dim):
    t = jnp.arange(n_tok)
    row = (t // GRID_W).astype(jnp.float32)
    col = (t % GRID_W).astype(jnp.float32)
    inv = ROPE_BASE ** (-jnp.arange(dim // 4, dtype=jnp.float32) / (dim // 4))
    return jnp.stack([row[:, None] * inv, col[:, None] * inv], axis=1)


def apply_axial_rope(x, ang):
    shp = x.shape
    xr = x.reshape(shp[:-1] + (2, 2, shp[-1] // 4)).astype(jnp.float32)
    x1, x2 = xr[..., 0, :], xr[..., 1, :]
    cos, sin = jnp.cos(ang), jnp.sin(ang)
    out = jnp.stack([x1 * cos - x2 * sin, x2 * cos + x1 * sin], axis=-2)
    return out.reshape(shp).astype(x.dtype)


def centred_dwconv(x, w):
    K = w.shape[0]
    return lax.conv_general_dilated(x, w[:, None, :], window_strides=(1,), padding=[(K // 2, K // 2)],
                                    dimension_numbers=('NWC', 'WIO', 'NWC'), feature_group_count=x.shape[-1])


def mlstm_prep(q, k, v, gates, conv_w, gate_b):
    qk = jax.nn.silu(centred_dwconv(jnp.concatenate([q, k], axis=-1), conv_w))
    q, k = jnp.split(qk, 2, axis=-1)
    f32 = jnp.float32
    q = to_heads(q, MLSTM_HEADS).astype(f32)
    k = to_heads(k, MLSTM_HEADS).astype(f32) * HEAD_DIM ** -0.5
    v = to_heads(v, MLSTM_HEADS).astype(f32)
    B, T, _ = gates.shape
    g = (gates.reshape(B, T, 4, MLSTM_HEADS) + gate_b).astype(f32).transpose(2, 0, 3, 1)
    fwd = (g[0], jax.nn.log_sigmoid(g[1]))
    bwd = (g[2], jax.nn.log_sigmoid(g[3]))
    return q, k, v, fwd, bwd


def mlstm_scan(q, k, v, log_i, log_f, state):
    B, H, T, d = q.shape
    nc = T // MLSTM_CHUNK

    def chunks(a):
        return jnp.moveaxis(a.reshape(a.shape[:2] + (nc, MLSTM_CHUNK) + a.shape[3:]), 2, 0)

    lower = jnp.tril(jnp.ones((MLSTM_CHUNK, MLSTM_CHUNK), dtype=bool))

    def step(carry, inp):
        C, n, m = carry
        qc, kc, vc, li, lf = inp
        b = jnp.cumsum(lf, axis=-1)
        dmat = jnp.where(lower, b[..., :, None] - b[..., None, :] + li[..., None, :], -jnp.inf)
        inter = b + m[..., None]
        m_t = jnp.maximum(inter, jnp.max(dmat, axis=-1))
        w_in = jnp.exp(dmat - m_t[..., None])
        w_st = jnp.exp(inter - m_t)
        s = jnp.einsum('bhtd,bhsd->bhts', qc, kc) * w_in
        num = w_st[..., None] * jnp.einsum('bhtd,bhde->bhte', qc, C) + jnp.einsum('bhts,bhse->bhte', s, vc)
        den = w_st * jnp.einsum('bhtd,bhd->bht', qc, n) + jnp.sum(s, axis=-1)
        h = num / jnp.maximum(jnp.abs(den), jnp.exp(-m_t))[..., None]
        b_end = b[..., -1]
        g = b_end[..., None] - b + li
        m_new = jnp.maximum(b_end + m, jnp.max(g, axis=-1))
        w_old = jnp.exp(b_end + m - m_new)
        w_tok = jnp.exp(g - m_new[..., None])
        C = w_old[..., None, None] * C + jnp.einsum('bhsd,bhse->bhde', kc * w_tok[..., None], vc)
        n = w_old[..., None] * n + jnp.einsum('bhs,bhsd->bhd', w_tok, kc)
        return (C, n, m_new), h

    state, h = lax.scan(step, state, (chunks(q), chunks(k), chunks(v), chunks(log_i), chunks(log_f)))
    return state, jnp.moveaxis(h, 0, 2).reshape(B, H, T, d)


def mlstm_bidir(q, k, v, gf, gb, st_f, st_b):
    st_f, h_f = mlstm_scan(q, k, v, gf[0], gf[1], st_f)
    rev = lambda a: jnp.flip(a, axis=2)
    st_b, h_b = mlstm_scan(rev(q), rev(k), rev(v), rev(gb[0]), rev(gb[1]), st_b)
    return st_f, st_b, h_f + rev(h_b)


def mlstm_mixer(lat, cx, conv_w, gate_b, head_g, need_ctx):
    q_l, k_l, v_l, gf_l, gb_l = mlstm_prep(lat[0], lat[1], lat[2], lat[4], conv_w, gate_b)
    q_c, k_c, v_c, gf_c, gb_c = mlstm_prep(cx[0], cx[1], cx[2], cx[4], conv_w, gate_b)
    B = q_c.shape[0]
    zero = (jnp.zeros((B, MLSTM_HEADS, HEAD_DIM, HEAD_DIM), jnp.float32),
            jnp.zeros((B, MLSTM_HEADS, HEAD_DIM), jnp.float32),
            jnp.zeros((B, MLSTM_HEADS), jnp.float32))
    st_f, st_b, h_c = mlstm_bidir(q_c, k_c, v_c, gf_c, gb_c, zero, zero)
    _, _, h_l = mlstm_bidir(q_l, k_l, v_l, gf_l, gb_l, st_f, st_b)
    g = head_g.reshape(MLSTM_HEADS, 1, HEAD_DIM)

    def finish(h, o):
        return from_heads(rms_norm(h, g)).astype(o.dtype) * jax.nn.sigmoid(o)

    return finish(h_l, lat[3]), (finish(h_c, cx[3]) if need_ctx else None)


def diff_heads_qk(t):
    B, T, _ = t.shape
    return t.reshape(B, T, DIFF_HEADS, 2, DIFF_QK_DIM).transpose(0, 3, 2, 1, 4)


def diff_scores_out(q, k, v, lam):
    s = jnp.einsum('bghqd,bghkd->bghqk', q, k).astype(jnp.float32) * DIFF_QK_DIM ** -0.5
    p = jax.nn.softmax(s, axis=-1)
    w = (p[:, 0] - lam * p[:, 1]).astype(v.dtype)
    return jnp.einsum('bhqk,bhkd->bhqd', w, v)


def diff_attention(lat, cx, lam_p, subln_g, layer_idx, need_ctx):
    q_l, k_l, v_l = lat
    q_c, k_c, v_c = cx
    B, S, _ = q_l.shape
    ang = axial_rope_angles(S, DIFF_QK_DIM)
    q_l = apply_axial_rope(diff_heads_qk(q_l), ang)
    k_l = apply_axial_rope(diff_heads_qk(k_l), ang)
    q_c, k_c = diff_heads_qk(q_c), diff_heads_qk(k_c)
    v_l, v_c = to_heads(v_l, DIFF_HEADS), to_heads(v_c, DIFF_HEADS)
    lam_init = 0.8 - 0.6 * math.exp(-0.3 * layer_idx)
    lp = lam_p.astype(jnp.float32)
    lam = jnp.exp(jnp.sum(lp[0] * lp[1])) - jnp.exp(jnp.sum(lp[2] * lp[3])) + lam_init
    k_all = jnp.concatenate([k_l, k_c], axis=3)
    v_all = jnp.concatenate([v_l, v_c], axis=2)
    nb = S // ATTN_BLOCK
    qb = jnp.moveaxis(q_l.reshape(B, 2, DIFF_HEADS, nb, ATTN_BLOCK, DIFF_QK_DIM), 3, 0)
    ob = lax.map(lambda qi: diff_scores_out(qi, k_all, v_all, lam), qb)
    o_l = jnp.moveaxis(ob, 0, 2).reshape(B, DIFF_HEADS, S, HEAD_DIM)

    def finish(o):
        return from_heads(rms_norm(o, subln_g) * (1 - lam_init))

    o_c = finish(diff_scores_out(q_c, k_c, v_c, lam)) if need_ctx else None
    return finish(o_l), o_c


def na_attention(lat, cx, rpb, need_ctx):
    scale = HEAD_DIM ** -0.5
    q_l = to_heads(lat[0], NA_HEADS) * scale
    k_l, v_l = to_heads(lat[1], NA_HEADS), to_heads(lat[2], NA_HEADS)
    q_c = to_heads(cx[0], NA_HEADS) * scale
    k_c, v_c = to_heads(cx[1], NA_HEADS), to_heads(cx[2], NA_HEADS)
    B, H, S, d = q_l.shape
    rows = S // GRID_W
    kr = min(NA_WIN_ROWS, rows)
    qg = q_l.reshape(B, H, rows, GRID_W, d)
    kg = k_l.reshape(B, H, rows, GRID_W, d)
    vg = v_l.reshape(B, H, rows, GRID_W, d)
    cols = jnp.arange(GRID_W)
    c_start = jnp.clip(cols - NA_WIN_COLS // 2, 0, GRID_W - NA_WIN_COLS)
    c_idx = c_start[:, None] + jnp.arange(NA_WIN_COLS)
    c_bias = c_idx - cols[:, None] + (NA_WIN_COLS - 1)
    n_win = kr * NA_WIN_COLS

    def row_fn(args):
        q_row, r = args
        rs = jnp.clip(r - kr // 2, 0, rows - kr)
        k_band = lax.dynamic_slice_in_dim(kg, rs, kr, axis=2)
        v_band = lax.dynamic_slice_in_dim(vg, rs, kr, axis=2)
        k_win = k_band[:, :, :, c_idx]
        v_win = v_band[:, :, :, c_idx]
        r_bias = rs + jnp.arange(kr) - r + (NA_WIN_ROWS - 1)
        bias = rpb[:, r_bias[:, None, None], c_bias[None]].transpose(0, 2, 1, 3)
        s_win = jnp.einsum('bhwd,bhrwcd->bhwrc', q_row, k_win).astype(jnp.float32) + bias.astype(jnp.float32)
        s_ctx = jnp.einsum('bhwd,bhld->bhwl', q_row, k_c).astype(jnp.float32)
        s = jnp.concatenate([s_win.reshape(B, H, GRID_W, n_win), s_ctx], axis=-1)
        p = jax.nn.softmax(s, axis=-1).astype(v_l.dtype)
        p_win = p[..., :n_win].reshape(B, H, GRID_W, kr, NA_WIN_COLS)
        return (jnp.einsum('bhwrc,bhrwcd->bhwd', p_win, v_win)
                + jnp.einsum('bhwl,bhld->bhwd', p[..., n_win:], v_c))

    o = lax.map(row_fn, (jnp.moveaxis(qg, 2, 0), jnp.arange(rows)))
    o_l = from_heads(jnp.moveaxis(o, 0, 2).reshape(B, H, S, d))
    o_c = None
    if need_ctx:
        pc = jax.nn.softmax(jnp.einsum('bhqd,bhkd->bhqk', q_c, k_c).astype(jnp.float32), axis=-1)
        o_c = from_heads(jnp.einsum('bhqk,bhkd->bhqd', pc.astype(v_c.dtype), v_c))
    return o_l, o_c


def even_mixer(ux, uy, w_in, w_out, conv_w, gate_b, head_g, lam_p, subln_g, layer_idx, need_ctx):
    px = split_cols(ux @ w_in, EVEN_SPLIT)
    py = split_cols(uy @ w_in, EVEN_SPLIT)
    ha_x, ha_y = mlstm_mixer(px[:5], py[:5], conv_w, gate_b, head_g, need_ctx)
    hb_x, hb_y = diff_attention(px[5:], py[5:], lam_p, subln_g, layer_idx, need_ctx)
    out_x = jnp.concatenate([ha_x, hb_x], axis=-1) @ w_out
    out_y = jnp.concatenate([ha_y, hb_y], axis=-1) @ w_out if need_ctx else None
    return out_x, out_y


def odd_mixer(ux, uy, w_in, w_out, rpb, need_ctx):
    px = jnp.split(ux @ w_in, 3, axis=-1)
    py = jnp.split(uy @ w_in, 3, axis=-1)
    o_x, o_y = na_attention(px, py, rpb, need_ctx)
    return o_x @ w_out, (o_y @ w_out if need_ctx else None)


def setup_inputs(seed: int = 0) -> dict:
    key = jax.random.key(seed)
    ks = jax.random.split(key, 20)
    nrm = jax.random.normal
    f32 = jnp.float32
    D = D_MODEL
    lin = jnp.linspace(3.0, 6.0, MLSTM_HEADS)
    zer = jnp.zeros((MLSTM_HEADS,), f32)
    gate_base = jnp.stack([zer, lin, zer, lin])
    return {
        'x': nrm(ks[0], (BATCH, SEQ, D), f32),
        'c': nrm(ks[1], (BATCH, D), f32),
        'ctx': nrm(ks[2], (BATCH, CTX_LEN, D), f32),
        'c_ctx': nrm(ks[3], (D,), f32),
        'w_ada': nrm(ks[4], (DEPTH, D, N_MOD * D), f32) * (0.5 * D ** -0.5),
        'b_ada': nrm(ks[5], (DEPTH, N_MOD * D), f32) * 0.02,
        'norm_g': 1.0 + 0.05 * nrm(ks[6], (DEPTH, 6, D), f32),
        'w_ffn_in': nrm(ks[7], (DEPTH, 2, D, 2 * D_FF), f32) * D ** -0.5,
        'w_ffn_out': nrm(ks[8], (DEPTH, 2, D_FF, D), f32) * D_FF ** -0.5,
        'w_in_even': nrm(ks[9], (N_EVEN, D, EVEN_IN), f32) * D ** -0.5,
        'w_out_even': nrm(ks[10], (N_EVEN, MLSTM_W + DIFF_W, D), f32) * (MLSTM_W + DIFF_W) ** -0.5,
        'mlstm_conv': nrm(ks[11], (N_EVEN, MLSTM_CONV, 2 * MLSTM_W), f32) * MLSTM_CONV ** -0.5,
        'mlstm_gate_b': gate_base[None] + 0.1 * nrm(ks[12], (N_EVEN, 4, MLSTM_HEADS), f32),
        'mlstm_head_g': 1.0 + 0.05 * nrm(ks[13], (N_EVEN, MLSTM_W), f32),
        'diff_lambda': 0.1 * nrm(ks[14], (N_EVEN, 4, DIFF_QK_DIM), f32),
        'diff_subln_g': 1.0 + 0.05 * nrm(ks[15], (N_EVEN, HEAD_DIM), f32),
        'w_in_odd': nrm(ks[16], (N_ODD, D, ODD_IN), f32) * D ** -0.5,
        'w_out_odd': nrm(ks[17], (N_ODD, NA_HEADS * HEAD_DIM, D), f32) * (NA_HEADS * HEAD_DIM) ** -0.5,
        'na_rpb': 0.1 * nrm(ks[18], (N_ODD, NA_HEADS, 2 * NA_WIN_ROWS - 1, 2 * NA_WIN_COLS - 1), f32),
    }


def reference(x, c, ctx, c_ctx, w_ada, b_ada, norm_g, w_ffn_in, w_ffn_out, w_in_even, w_out_even,
              mlstm_conv, mlstm_gate_b, mlstm_head_g, diff_lambda, diff_subln_g, w_in_odd, w_out_odd, na_rpb):
    B = x.shape[0]
    y = ctx
    for l in range(DEPTH):
        last = l == DEPTH - 1
        mx = (jax.nn.silu(c) @ w_ada[l] + b_ada[l]).reshape(B, N_MOD, 1, D_MODEL)
        my = (jax.nn.silu(c_ctx) @ w_ada[l] + b_ada[l]).reshape(N_MOD, D_MODEL)
        g = norm_g[l]
        x = ffn_sublayer(x, (mx[:, 0], mx[:, 1], mx[:, 2]), g[0], g[1], w_ffn_in[l, 0], w_ffn_out[l, 0])
        y = ffn_sublayer(y, (my[0], my[1], my[2]), g[0], g[1], w_ffn_in[l, 0], w_ffn_out[l, 0])
        ux = rms_norm(x, g[2]) * (1 + mx[:, 4]) + mx[:, 3]
        uy = rms_norm(y, g[2]) * (1 + my[4]) + my[3]
        if l % 2 == 0:
            e = l // 2
            mix_x, mix_y = even_mixer(ux, uy, w_in_even[e], w_out_even[e], mlstm_conv[e], mlstm_gate_b[e],
                                      mlstm_head_g[e], diff_lambda[e], diff_subln_g[e], l, not last)
        else:
            o = l // 2
            mix_x, mix_y = odd_mixer(ux, uy, w_in_odd[o], w_out_odd[o], na_rpb[o], not last)
        x = x + mx[:, 5] * rms_norm(mix_x, g[3])
        x = ffn_sublayer(x, (mx[:, 6], mx[:, 7], mx[:, 8]), g[4], g[5], w_ffn_in[l, 1], w_ffn_out[l, 1])
        if not last:
            y = y + my[5] * rms_norm(mix_y, g[3])
            y = ffn_sublayer(y, (my[6], my[7], my[8]), g[4], g[5], w_ffn_in[l, 1], w_ffn_out[l, 1])
    return x
```

```python
import functools
import math

import jax
import jax.numpy as jnp
from jax import lax
from jax.experimental import pallas as pl
from jax.experimental.pallas import tpu as pltpu

F32 = jnp.float32
BF16 = jnp.bfloat16

D_MODEL = 2048
BATCH = 4
SEQ = 2048
DEPTH = 2
GRID_W = 64
GRID_H = SEQ // GRID_W
CTX_LEN = 256
HEAD_DIM = 128
N_HEADS = D_MODEL // HEAD_DIM
MLSTM_HEADS = N_HEADS // 2
DIFF_HEADS = N_HEADS - MLSTM_HEADS
MLSTM_W = MLSTM_HEADS * HEAD_DIM
DIFF_W = DIFF_HEADS * HEAD_DIM
DIFF_QK_DIM = HEAD_DIM // 2
MLSTM_CHUNK = 128
MLSTM_CONV = 5
NA_HEADS = N_HEADS
NA_WIN_ROWS = 8
NA_WIN_COLS = 16
D_FF = 5632
ROPE_BASE = 10000.0
NORM_EPS = 1e-6
FFN_RES = 0.5
N_MOD = 9

T_LAT = BATCH * SEQ
T_CTX = BATCH * CTX_LEN
T_ALL = T_LAT + T_CTX

LANES = 128
VMEM_LIMIT = 56 * 1024 * 1024
TM = 1024
ROW_CHUNK = 32
N_LAT_TILES = T_LAT // TM
N_ALL_TILES = T_ALL // TM
TILES_PER_BATCH = SEQ // TM
FFN_TN = 512
FFN_TK = 512
PROJ_TN = 512
PREP_ROWS = 256
N_PREP_LAT = T_LAT // PREP_ROWS
PREP_PER_SEQ = SEQ // PREP_ROWS
HALO = 8
DIFF_TQ = 256
NA_GROUP_ROWS = 8
NA_TQ = NA_GROUP_ROWS * GRID_W
NA_BAND_ROWS = 16
NA_BAND = NA_BAND_ROWS * GRID_W
NA_GROUPS = GRID_H // NA_GROUP_ROWS
NA_DR = 2 * NA_WIN_ROWS

assert CTX_LEN * BATCH == TM and SEQ % TM == 0


def _mod_index(i):
    return jnp.minimum(i // TILES_PER_BATCH, BATCH)


def _cparams(sem):
    return pltpu.CompilerParams(dimension_semantics=sem, vmem_limit_bytes=VMEM_LIMIT)


def _dot(a, b):
    return jnp.dot(a, b, preferred_element_type=F32)


def _dot_nt(a, b):
    return lax.dot_general(a, b, (((1,), (1,)), ((), ())), preferred_element_type=F32)


def _dot_tn(a, b):
    return lax.dot_general(a, b, (((0,), (0,)), ((), ())), preferred_element_type=F32)


def _sigmoid(x):
    return 1.0 / (1.0 + jnp.exp(-x))


def _silu(x):
    return x * _sigmoid(x)


def _log_sigmoid(x):
    return jnp.minimum(x, 0.0) - jnp.log(1.0 + jnp.exp(-jnp.abs(x)))


def _ada_kernel(c_ref, w_ref, b_ref, o_ref):
    a = _silu(c_ref[...]).astype(BF16)
    o_ref[0] = _dot(a, w_ref[0].astype(BF16)) + b_ref[0]


def _ada_mods(c8, w_ada, b_ada):
    tn = 1024
    n = N_MOD * D_MODEL
    out = pl.pallas_call(
        _ada_kernel,
        out_shape=jax.ShapeDtypeStruct((DEPTH, 8, n), F32),
        grid=(DEPTH, n // tn),
        in_specs=[
            pl.BlockSpec((8, D_MODEL), lambda l, j: (0, 0)),
            pl.BlockSpec((1, D_MODEL, tn), lambda l, j: (l, 0, j)),
            pl.BlockSpec((1, 1, tn), lambda l, j: (l, 0, j)),
        ],
        out_specs=pl.BlockSpec((1, 8, tn), lambda l, j: (l, 0, j)),
        compiler_params=_cparams(("parallel", "parallel")),
        name="ada_mods",
    )(c8, w_ada, b_ada.reshape(DEPTH, 1, n))
    return out[:, :BATCH + 1].reshape(DEPTH, BATCH + 1, N_MOD, D_MODEL)


def _norm_mod_rows(x_ref, u_ref, g, scale, shift):
    def body(r, carry):
        rows = pl.ds(pl.multiple_of(r * ROW_CHUNK, ROW_CHUNK), ROW_CHUNK)
        x = x_ref[rows, :]
        ms = jnp.mean(x * x, axis=-1, keepdims=True)
        u = x * lax.rsqrt(ms + NORM_EPS) * g
        u_ref[rows, :] = (u * (1.0 + scale) + shift).astype(BF16)
        return carry

    lax.fori_loop(0, TM // ROW_CHUNK, body, 0)


def _ffn_in_kernel(x_ref, mod_ref, g_ref, wg_ref, wu_ref, o_ref, u_ref, *, shift_row, scale_row):
    @pl.when(pl.program_id(1) == 0)
    def _():
        _norm_mod_rows(x_ref, u_ref, g_ref[...], mod_ref[0, scale_row:scale_row + 1, :],
                       mod_ref[0, shift_row:shift_row + 1, :])

    u = u_ref[...]
    gt = _dot(u, wg_ref[...])
    up = _dot(u, wu_ref[...])
    o_ref[...] = (_silu(gt) * up).astype(o_ref.dtype)


def _ffn_in(h, mods, g_pre, w_in, shift_row, n_tiles):
    nff = D_FF // FFN_TN
    kern = functools.partial(_ffn_in_kernel, shift_row=shift_row, scale_row=shift_row + 1)
    return pl.pallas_call(
        kern,
        out_shape=jax.ShapeDtypeStruct((n_tiles * TM, D_FF), BF16),
        grid=(n_tiles, nff),
        in_specs=[
            pl.BlockSpec((TM, D_MODEL), lambda i, j: (i, 0)),
            pl.BlockSpec((1, N_MOD, D_MODEL), lambda i, j: (_mod_index(i), 0, 0)),
            pl.BlockSpec((1, D_MODEL), lambda i, j: (0, 0)),
            pl.BlockSpec((D_MODEL, FFN_TN), lambda i, j: (0, j)),
            pl.BlockSpec((D_MODEL, FFN_TN), lambda i, j: (0, j + nff)),
        ],
        out_specs=pl.BlockSpec((TM, FFN_TN), lambda i, j: (i, j)),
        scratch_shapes=[pltpu.VMEM((TM, D_MODEL), BF16)],
        compiler_params=_cparams(("parallel", "arbitrary")),
        name="ffn_in",
    )(h, mods, g_pre.reshape(1, D_MODEL), w_in, w_in)


def _proj_kernel(x_ref, mod_ref, g_ref, w_ref, o_ref, u_ref, *, shift_row, scale_row, n_scaled, out_scale):
    j = pl.program_id(1)

    @pl.when(j == 0)
    def _():
        _norm_mod_rows(x_ref, u_ref, g_ref[...], mod_ref[0, scale_row:scale_row + 1, :],
                       mod_ref[0, shift_row:shift_row + 1, :])

    acc = _dot(u_ref[...], w_ref[...])
    if n_scaled:
        acc = acc * jnp.where(j < n_scaled, out_scale, 1.0).astype(F32)
    o_ref[...] = acc.astype(o_ref.dtype)


def _proj(h, mods, g_pre, w, shift_row, out_dtype, tn, n_scaled=0, out_scale=1.0):
    n = w.shape[1]
    kern = functools.partial(_proj_kernel, shift_row=shift_row, scale_row=shift_row + 1,
                             n_scaled=n_scaled, out_scale=out_scale)
    return pl.pallas_call(
        kern,
        out_shape=jax.ShapeDtypeStruct((T_ALL, n), out_dtype),
        grid=(N_ALL_TILES, n // tn),
        in_specs=[
            pl.BlockSpec((TM, D_MODEL), lambda i, j: (i, 0)),
            pl.BlockSpec((1, N_MOD, D_MODEL), lambda i, j: (_mod_index(i), 0, 0)),
            pl.BlockSpec((1, D_MODEL), lambda i, j: (0, 0)),
            pl.BlockSpec((D_MODEL, tn), lambda i, j: (0, j)),
        ],
        out_specs=pl.BlockSpec((TM, tn), lambda i, j: (i, j)),
        scratch_shapes=[pltpu.VMEM((TM, D_MODEL), BF16)],
        compiler_params=_cparams(("parallel", "arbitrary")),
        name="proj",
    )(h, mods, g_pre.reshape(1, D_MODEL), w)


def _out_kernel(a_ref, w_ref, h_ref, mod_ref, g_ref, o_ref, *, gate_row, coef, nk):
    k = pl.program_id(1)

    @pl.when(k == 0)
    def _():
        o_ref[...] = _dot(a_ref[...], w_ref[...])

    @pl.when(k > 0)
    def _():
        o_ref[...] += _dot(a_ref[...], w_ref[...])

    @pl.when(k == nk - 1)
    def _():
        g = g_ref[...]
        gate = mod_ref[0, gate_row:gate_row + 1, :] * coef

        def body(r, carry):
            rows = pl.ds(pl.multiple_of(r * ROW_CHUNK, ROW_CHUNK), ROW_CHUNK)
            y = o_ref[rows, :]
            ms = jnp.mean(y * y, axis=-1, keepdims=True)
            o_ref[rows, :] = h_ref[rows, :] + gate * (y * lax.rsqrt(ms + NORM_EPS) * g)
            return carry

        lax.fori_loop(0, TM // ROW_CHUNK, body, 0)


def _out_proj(a, w, h, mods, g_post, gate_row, coef, n_tiles, tk):
    kdim = a.shape[1]
    nk = kdim // tk
    kern = functools.partial(_out_kernel, gate_row=gate_row, coef=coef, nk=nk)
    return pl.pallas_call(
        kern,
        out_shape=jax.ShapeDtypeStruct((n_tiles * TM, D_MODEL), F32),
        grid=(n_tiles, nk),
        in_specs=[
            pl.BlockSpec((TM, tk), lambda i, k: (i, k)),
            pl.BlockSpec((tk, D_MODEL), lambda i, k: (k, 0)),
            pl.BlockSpec((TM, D_MODEL), lambda i, k: (i, 0)),
            pl.BlockSpec((1, N_MOD, D_MODEL), lambda i, k: (_mod_index(i), 0, 0)),
            pl.BlockSpec((1, D_MODEL), lambda i, k: (0, 0)),
        ],
        out_specs=pl.BlockSpec((TM, D_MODEL), lambda i, k: (i, 0)),
        compiler_params=_cparams(("parallel", "arbitrary")),
        name="out_proj",
    )(a, w, h, mods, g_post.reshape(1, D_MODEL))


def _conv_kernel(x_ref, p_ref, n_ref, w_ref, o_ref):
    rb = pl.program_id(0)
    j = pl.program_id(1)
    in_lat = rb < N_PREP_LAT
    has_prev = jnp.logical_and(in_lat, rb % PREP_PER_SEQ != 0)
    has_next = jnp.logical_and(in_lat, rb % PREP_PER_SEQ != PREP_PER_SEQ - 1)
    x = x_ref[...]
    prev = jnp.where(has_prev, p_ref[...], 0.0)
    nxt = jnp.where(has_next, n_ref[...], 0.0)
    row = lax.broadcasted_iota(jnp.int32, x.shape, 0)
    w = w_ref[...]
    half = MLSTM_CONV // 2
    acc = x * w[half:half + 1, :]
    for s in range(1, half + 1):
        back = pltpu.roll(x, s, axis=0)
        fwd = pltpu.roll(x, PREP_ROWS - s, axis=0)
        for e in range(s):
            back = jnp.where(row == e, prev[HALO - s + e:HALO - s + e + 1, :], back)
            fwd = jnp.where(row == PREP_ROWS - s + e, nxt[e:e + 1, :], fwd)
        acc = acc + back * w[half - s:half - s + 1, :] + fwd * w[half + s:half + s + 1, :]
    y = _silu(acc)
    y = y * jnp.where(j == 1, HEAD_DIM ** -0.5, 1.0).astype(F32)
    o_ref[...] = y.astype(o_ref.dtype)


def _mlstm_conv(proj, conv_w):
    c = MLSTM_W
    nblk = T_ALL // PREP_ROWS
    per = PREP_ROWS // HALO
    return pl.pallas_call(
        _conv_kernel,
        out_shape=jax.ShapeDtypeStruct((T_ALL, 2 * MLSTM_W), BF16),
        grid=(nblk, 2),
        in_specs=[
            pl.BlockSpec((PREP_ROWS, c), lambda rb, j: (rb, j)),
            pl.BlockSpec((HALO, c), lambda rb, j: (jnp.maximum(rb * per - 1, 0), j)),
            pl.BlockSpec((HALO, c), lambda rb, j: (jnp.minimum((rb + 1) * per, T_ALL // HALO - 1), j)),
            pl.BlockSpec((MLSTM_CONV, c), lambda rb, j: (0, j)),
        ],
        out_specs=pl.BlockSpec((PREP_ROWS, c), lambda rb, j: (rb, j)),
        compiler_params=_cparams(("parallel", "parallel")),
        name="mlstm_conv",
    )(proj, proj, proj, conv_w)


def _rope_kernel(x_ref, cos_ref, sin_ref, o_ref):
    j = pl.program_id(1)
    cos = cos_ref[...]
    sin = sin_ref[...]
    lane = lax.broadcasted_iota(jnp.int32, cos.shape, 1)
    nf = DIFF_QK_DIM // 4
    first = (lane % (2 * nf)) < nf
    scale = jnp.where(j == 0, DIFF_QK_DIM ** -0.5, 1.0).astype(F32)
    for hh in range(DIFF_HEADS):
        x = x_ref[:, hh * HEAD_DIM:(hh + 1) * HEAD_DIM]
        partner = jnp.where(first, pltpu.roll(x, HEAD_DIM - nf, axis=1), pltpu.roll(x, nf, axis=1))
        y = (x * cos + partner * sin) * scale
        o_ref[:, hh * HEAD_DIM:(hh + 1) * HEAD_DIM] = y.astype(o_ref.dtype)


def _rope_tables():
    t = jnp.arange(SEQ)
    row = (t // GRID_W).astype(F32)
    col = (t % GRID_W).astype(F32)
    nf = DIFF_QK_DIM // 4
    inv = ROPE_BASE ** (-jnp.arange(nf, dtype=F32) / nf)
    cos64 = jnp.concatenate([jnp.cos(a[:, None] * inv) for a in (row, row, col, col)], axis=1)
    sin64 = jnp.concatenate([sg * jnp.sin(a[:, None] * inv)
                             for sg, a in ((-1.0, row), (1.0, row), (-1.0, col), (1.0, col))], axis=1)
    cos_t = jnp.concatenate([cos64, cos64], axis=1)
    sin_t = jnp.concatenate([sin64, sin64], axis=1)
    cos_t = jnp.concatenate([cos_t, jnp.ones((PREP_ROWS, HEAD_DIM), F32)], axis=0)
    sin_t = jnp.concatenate([sin_t, jnp.zeros((PREP_ROWS, HEAD_DIM), F32)], axis=0)
    return cos_t, sin_t


def _diff_rope(proj, col_block0):
    cos_t, sin_t = _rope_tables()
    nblk = T_ALL // PREP_ROWS

    def tab_map(rb, j):
        return (jnp.where(rb < N_PREP_LAT, rb % PREP_PER_SEQ, PREP_PER_SEQ), 0)

    return pl.pallas_call(
        _rope_kernel,
        out_shape=jax.ShapeDtypeStruct((T_ALL, 2 * DIFF_W), BF16),
        grid=(nblk, 2),
        in_specs=[
            pl.BlockSpec((PREP_ROWS, DIFF_W), lambda rb, j: (rb, col_block0 + j)),
            pl.BlockSpec((PREP_ROWS, HEAD_DIM), tab_map),
            pl.BlockSpec((PREP_ROWS, HEAD_DIM), tab_map),
        ],
        out_specs=pl.BlockSpec((PREP_ROWS, DIFF_W), lambda rb, j: (rb, j)),
        compiler_params=_cparams(("parallel", "parallel")),
        name="diff_rope",
    )(proj, cos_t, sin_t)


def _scan_incl(x, axis, reverse):
    n = x.shape[axis]
    idx = lax.broadcasted_iota(jnp.int32, x.shape, axis)
    s = 1
    while s < n:
        if reverse:
            x = x + jnp.where(idx < n - s, pltpu.roll(x, n - s, axis=axis), 0.0)
        else:
            x = x + jnp.where(idx >= s, pltpu.roll(x, s, axis=axis), 0.0)
        s *= 2
    return x


def _mlstm_kernel(*refs, reverse):
    if reverse:
        q_ref, k_ref, v_ref, g_ref, gb_ref, hf_ref, o_ref, hg_ref, out_ref, c_ref, n_ref, m_ref = refs
    else:
        q_ref, k_ref, v_ref, g_ref, gb_ref, out_ref, c_ref, n_ref, m_ref = refs
    L = MLSTM_CHUNK

    @pl.when(pl.program_id(1) == 0)
    def _():
        c_ref[...] = jnp.zeros_like(c_ref)
        n_ref[...] = jnp.zeros_like(n_ref)
        m_ref[...] = jnp.zeros_like(m_ref)

    g = g_ref[...] + gb_ref[...]
    gt = g.T
    cum_col = _scan_incl(_log_sigmoid(g), 0, reverse)
    cum_row = _scan_incl(_log_sigmoid(gt), 1, reverse)
    ti = lax.broadcasted_iota(jnp.int32, (L, L), 0)
    si = lax.broadcasted_iota(jnp.int32, (L, L), 1)
    causal = (si >= ti) if reverse else (si <= ti)
    base = 2 * MLSTM_HEADS if reverse else 0
    end = 0 if reverse else L - 1

    for hh in range(MLSTM_HEADS):
        cols = slice(hh * HEAD_DIM, (hh + 1) * HEAD_DIM)
        ci = base + hh
        cf = base + MLSTM_HEADS + hh
        q = q_ref[:, cols]
        k = k_ref[:, cols]
        v = v_ref[:, cols].astype(BF16)
        b_col = cum_col[:, cf:cf + 1]
        b_row = cum_row[cf:cf + 1, :]
        li_col = g[:, ci:ci + 1]
        li_row = gt[ci:ci + 1, :]
        m_old = m_ref[hh:hh + 1, 0:1]
        n_old = n_ref[hh:hh + 1, :]
        c_old = c_ref[hh]

        dmat = jnp.where(causal, b_col - b_row + li_row, -jnp.inf)
        inter = b_col + m_old
        m_t = jnp.maximum(inter, jnp.max(dmat, axis=1, keepdims=True))
        w_in = jnp.exp(dmat - m_t)
        w_st = jnp.exp(inter - m_t)
        s = _dot_nt(q, k) * w_in
        num = w_st * _dot(q, c_old.astype(BF16)) + _dot(s.astype(BF16), v)
        qn = jnp.sum(q.astype(F32) * n_old, axis=1, keepdims=True)
        den = w_st * qn + jnp.sum(s, axis=1, keepdims=True)
        h = num / jnp.maximum(jnp.abs(den), jnp.exp(-m_t))

        b_end = b_col[end:end + 1, :]
        g_row = b_end - b_row + li_row
        g_col = b_end - b_col + li_col
        m_new = jnp.maximum(b_end + m_old, jnp.max(g_row, axis=1, keepdims=True))
        w_old = jnp.exp(b_end + m_old - m_new)
        kw = k.astype(F32) * jnp.exp(g_col - m_new)
        c_ref[hh] = w_old * c_old + _dot_tn(kw.astype(BF16), v)
        n_ref[hh:hh + 1, :] = w_old * n_old + jnp.sum(kw, axis=0, keepdims=True)
        m_ref[hh:hh + 1, :] = jnp.broadcast_to(m_new, (1, LANES))

        if reverse:
            tot = hf_ref[:, cols] + h
            ms = jnp.mean(tot * tot, axis=1, keepdims=True)
            y = tot * lax.rsqrt(ms + NORM_EPS) * hg_ref[:, cols]
            out_ref[:, cols] = (y * _sigmoid(o_ref[:, cols])).astype(out_ref.dtype)
        else:
            out_ref[:, cols] = h


def _mlstm_rowblock(reverse):
    nlat = SEQ // MLSTM_CHUNK
    nctx = CTX_LEN // MLSTM_CHUNK
    lat0 = T_LAT // MLSTM_CHUNK

    def rb(b, s):
        if reverse:
            return jnp.where(s < nctx, lat0 + nctx * b + (nctx - 1 - s), nlat * b + (nlat + nctx - 1 - s))
        return jnp.where(s < nctx, lat0 + nctx * b + s, nlat * b + (s - nctx))

    return rb


def _mlstm_scan(qk, proj, gates, gate_b_row, reverse, hf=None, head_g=None):
    rb = _mlstm_rowblock(reverse)
    steps = (SEQ + CTX_LEN) // MLSTM_CHUNK
    blk = lambda cb: pl.BlockSpec((MLSTM_CHUNK, MLSTM_W), lambda b, s: (rb(b, s), cb))
    in_specs = [blk(0), blk(1), blk(2),
                pl.BlockSpec((MLSTM_CHUNK, LANES), lambda b, s: (rb(b, s), 0)),
                pl.BlockSpec((1, LANES), lambda b, s: (0, 0))]
    args = [qk, qk, proj, gates, gate_b_row]
    if reverse:
        in_specs += [blk(0), blk(3), pl.BlockSpec((1, MLSTM_W), lambda b, s: (0, 0))]
        args += [hf, proj, head_g.reshape(1, MLSTM_W)]
    return pl.pallas_call(
        functools.partial(_mlstm_kernel, reverse=reverse),
        out_shape=jax.ShapeDtypeStruct((T_ALL, MLSTM_W), BF16 if reverse else F32),
        grid=(BATCH, steps),
        in_specs=in_specs,
        out_specs=blk(0),
        scratch_shapes=[pltpu.VMEM((MLSTM_HEADS, HEAD_DIM, HEAD_DIM), F32),
                        pltpu.VMEM((MLSTM_HEADS, HEAD_DIM), F32),
                        pltpu.VMEM((MLSTM_HEADS, LANES), F32)],
        compiler_params=_cparams(("parallel", "arbitrary")),
        name="mlstm_bwd" if reverse else "mlstm_fwd",
    )(*args)


def _diff_kernel(q_ref, kl_ref, kc_ref, vl_ref, vc_ref, lam_ref, g_ref, o_ref, *, lam_init):
    is_ctx = pl.program_id(2) == SEQ // DIFF_TQ
    lp = lam_ref[...]
    lam = (jnp.exp(jnp.sum(lp[0:1] * lp[1:2], axis=1, keepdims=True))
           - jnp.exp(jnp.sum(lp[2:3] * lp[3:4], axis=1, keepdims=True)) + lam_init)
    q = q_ref[...]
    lane = lax.broadcasted_iota(jnp.int32, q.shape, 1)
    kl = kl_ref[...]
    kc = kc_ref[...]
    vl = vl_ref[...].astype(BF16)
    vc = vc_ref[...].astype(BF16)
    probs = []
    for mp in range(2):
        sel = (lane < DIFF_QK_DIM) if mp == 0 else (lane >= DIFF_QK_DIM)
        qm = jnp.where(sel, q, jnp.zeros_like(q))
        sl = jnp.where(is_ctx, -jnp.inf, _dot_nt(qm, kl))
        sc = _dot_nt(qm, kc)
        mx = jnp.maximum(jnp.max(sl, axis=1, keepdims=True), jnp.max(sc, axis=1, keepdims=True))
        pl_ = jnp.exp(sl - mx)
        pc = jnp.exp(sc - mx)
        inv = 1.0 / (jnp.sum(pl_, axis=1, keepdims=True) + jnp.sum(pc, axis=1, keepdims=True))
        probs.append((pl_, pc, inv))
    (p0l, p0c, i0), (p1l, p1c, i1) = probs
    i1 = i1 * lam
    wl = (p0l * i0 - p1l * i1).astype(BF16)
    wc = (p0c * i0 - p1c * i1).astype(BF16)
    o = _dot(wl, vl) + _dot(wc, vc)
    ms = jnp.mean(o * o, axis=1, keepdims=True)
    y = o * lax.rsqrt(ms + NORM_EPS) * g_ref[...] * (1.0 - lam_init)
    o_ref[...] = y.astype(o_ref.dtype)


def _diff_attention(dqk, proj, v_block0, lam_p, subln_g, lam_init):
    nq = SEQ // DIFF_TQ
    ctx0 = T_LAT // CTX_LEN

    def q_map(b, h, qi):
        return (jnp.where(qi < nq, b * nq + qi, T_LAT // DIFF_TQ + b), h)

    return pl.pallas_call(
        functools.partial(_diff_kernel, lam_init=lam_init),
        out_shape=jax.ShapeDtypeStruct((T_ALL, DIFF_W), BF16),
        grid=(BATCH, DIFF_HEADS, nq + 1),
        in_specs=[
            pl.BlockSpec((DIFF_TQ, HEAD_DIM), q_map),
            pl.BlockSpec((SEQ, HEAD_DIM), lambda b, h, qi: (b, DIFF_HEADS + h)),
            pl.BlockSpec((CTX_LEN, HEAD_DIM), lambda b, h, qi: (ctx0 + b, DIFF_HEADS + h)),
            pl.BlockSpec((SEQ, HEAD_DIM), lambda b, h, qi: (b, v_block0 + h)),
            pl.BlockSpec((CTX_LEN, HEAD_DIM), lambda b, h, qi: (ctx0 + b, v_block0 + h)),
            pl.BlockSpec((4, DIFF_QK_DIM), lambda b, h, qi: (0, 0)),
            pl.BlockSpec((1, HEAD_DIM), lambda b, h, qi: (0, 0)),
        ],
        out_specs=pl.BlockSpec((DIFF_TQ, HEAD_DIM), q_map),
        compiler_params=_cparams(("parallel", "parallel", "arbitrary")),
        name="diff_attn",
    )(dqk, dqk, dqk, proj, proj, lam_p, subln_g.reshape(1, HEAD_DIM))


def _na_bias_kernel(rpb_ref, o_ref):
    h = pl.program_id(0)
    nrel_r = 2 * NA_WIN_ROWS - 1
    nrel_c = 2 * NA_WIN_COLS - 1
    shape = (GRID_W, 2 * GRID_W)
    c = lax.broadcasted_iota(jnp.int32, shape, 0)
    lane = lax.broadcasted_iota(jnp.int32, shape, 1)
    left = lane < GRID_W
    kc = jnp.where(left, lane, lane - GRID_W)
    cs = jnp.clip(c - NA_WIN_COLS // 2, 0, GRID_W - NA_WIN_COLS)
    in_win = jnp.logical_and(kc >= cs, kc < cs + NA_WIN_COLS)
    rel = kc - c + (NA_WIN_COLS - 1)
    for d in range(NA_DR):
        dl = min(max(d - 1, 0), nrel_r - 1)
        dr = min(d, nrel_r - 1)
        acc = jnp.zeros(shape, F32)
        for e in range(nrel_c):
            vl = rpb_ref[(h * nrel_r + dl) * nrel_c + e]
            vr = rpb_ref[(h * nrel_r + dr) * nrel_c + e]
            acc = jnp.where(rel == e, jnp.where(left, vl, vr), acc)
        o_ref[0, d] = jnp.where(in_win, acc, -jnp.inf)


def _na_bias_slabs(rpb):
    return pl.pallas_call(
        _na_bias_kernel,
        out_shape=jax.ShapeDtypeStruct((NA_HEADS, NA_DR, GRID_W, 2 * GRID_W), F32),
        grid=(NA_HEADS,),
        in_specs=[pl.BlockSpec(memory_space=pltpu.SMEM)],
        out_specs=pl.BlockSpec((1, NA_DR, GRID_W, 2 * GRID_W), lambda h: (h, 0, 0, 0)),
        compiler_params=_cparams(("parallel",)),
        name="na_bias",
    )(rpb.reshape(-1))


def _na_kernel(q_ref, k_ref, v_ref, kc_ref, vc_ref, slab_ref, o_ref, bias_ref):
    g = pl.program_id(1)
    band0 = jnp.clip(NA_GROUP_ROWS * g - NA_WIN_ROWS // 2, 0, GRID_H - NA_BAND_ROWS)

    @pl.when(pl.program_id(2) == 0)
    def _():
        lane = lax.broadcasted_iota(jnp.int32, (GRID_W, 2 * GRID_W), 1)
        left = lane < GRID_W
        for i in range(NA_GROUP_ROWS):
            r = NA_GROUP_ROWS * g + i
            rs = jnp.clip(r - NA_WIN_ROWS // 2, 0, GRID_H - NA_WIN_ROWS)
            for jp in range(NA_BAND_ROWS // 2):
                kr0 = band0 + 2 * jp
                ok0 = jnp.logical_and(kr0 >= rs, kr0 < rs + NA_WIN_ROWS)
                ok1 = jnp.logical_and(kr0 + 1 >= rs, kr0 + 1 < rs + NA_WIN_ROWS)
                d = jnp.clip(kr0 - r + NA_WIN_ROWS, 0, NA_DR - 1)
                ok = jnp.where(left, ok0.astype(jnp.int32), ok1.astype(jnp.int32)) > 0
                bias_ref[i * GRID_W:(i + 1) * GRID_W, jp * 2 * GRID_W:(jp + 1) * 2 * GRID_W] = (
                    jnp.where(ok, slab_ref[0, d], -jnp.inf))

    start = pl.multiple_of(band0 * GRID_W, NA_WIN_ROWS // 2 * GRID_W)
    q = q_ref[...]
    kb = k_ref[pl.ds(start, NA_BAND), :]
    vb = v_ref[pl.ds(start, NA_BAND), :]
    s = _dot_nt(q, kb) + bias_ref[...]
    sc = _dot_nt(q, kc_ref[...])
    mx = jnp.maximum(jnp.max(s, axis=1, keepdims=True), jnp.max(sc, axis=1, keepdims=True))
    p = jnp.exp(s - mx)
    pc = jnp.exp(sc - mx)
    den = jnp.sum(p, axis=1, keepdims=True) + jnp.sum(pc, axis=1, keepdims=True)
    o = _dot(p.astype(BF16), vb) + _dot(pc.astype(BF16), vc_ref[...])
    o_ref[...] = (o / den).astype(o_ref.dtype)


def _na_attention(qkv, slabs):
    ctx0 = T_LAT // CTX_LEN
    return pl.pallas_call(
        _na_kernel,
        out_shape=jax.ShapeDtypeStruct((T_LAT, D_MODEL), BF16),
        grid=(NA_HEADS, NA_GROUPS, BATCH),
        in_specs=[
            pl.BlockSpec((NA_TQ, HEAD_DIM), lambda h, g, b: (b * NA_GROUPS + g, h)),
            pl.BlockSpec((SEQ, HEAD_DIM), lambda h, g, b: (b, NA_HEADS + h)),
            pl.BlockSpec((SEQ, HEAD_DIM), lambda h, g, b: (b, 2 * NA_HEADS + h)),
            pl.BlockSpec((CTX_LEN, HEAD_DIM), lambda h, g, b: (ctx0 + b, NA_HEADS + h)),
            pl.BlockSpec((CTX_LEN, HEAD_DIM), lambda h, g, b: (ctx0 + b, 2 * NA_HEADS + h)),
            pl.BlockSpec((1, NA_DR, GRID_W, 2 * GRID_W), lambda h, g, b: (h, 0, 0, 0)),
        ],
        out_specs=pl.BlockSpec((NA_TQ, HEAD_DIM), lambda h, g, b: (b * NA_GROUPS + g, h)),
        scratch_shapes=[pltpu.VMEM((NA_TQ, NA_BAND), F32)],
        compiler_params=_cparams(("parallel", "arbitrary", "arbitrary")),
        name="na_attn",
    )(qkv, qkv, qkv, qkv, qkv, slabs)


def _even_layer_mix(h, mods, g_pre, w_in, conv_w, gate_b, head_g, lam_p, subln_g, layer_idx):
    gate0 = 4 * MLSTM_W
    ngate = 4 * MLSTM_HEADS
    w_main = jnp.concatenate([w_in[:, :gate0], w_in[:, gate0 + ngate:]], axis=1).astype(BF16)
    w_gate = jnp.pad(w_in[:, gate0:gate0 + ngate], ((0, 0), (0, LANES - ngate))).astype(BF16)
    proj = _proj(h, mods, g_pre, w_main, 3, F32, PROJ_TN)
    gates = _proj(h, mods, g_pre, w_gate, 3, F32, LANES)
    gate_b_row = jnp.pad(gate_b.reshape(1, ngate), ((0, 0), (0, LANES - ngate)))
    qk = _mlstm_conv(proj, conv_w)
    hf = _mlstm_scan(qk, proj, gates, gate_b_row, reverse=False)
    ha = _mlstm_scan(qk, proj, gates, gate_b_row, reverse=True, hf=hf, head_g=head_g)
    dqk = _diff_rope(proj, 4 * MLSTM_W // DIFF_W)
    lam_init = 0.8 - 0.6 * math.exp(-0.3 * layer_idx)
    hb = _diff_attention(dqk, proj, (4 * MLSTM_W + 2 * DIFF_W) // HEAD_DIM, lam_p, subln_g, lam_init)
    return jnp.concatenate([ha, hb], axis=1)


def _odd_layer_mix(h, mods, g_pre, w_in, rpb):
    qkv = _proj(h, mods, g_pre, w_in.astype(BF16), 3, BF16, PROJ_TN,
                n_scaled=D_MODEL // PROJ_TN, out_scale=HEAD_DIM ** -0.5)
    return _na_attention(qkv, _na_bias_slabs(rpb))


def kernel(x, c, ctx, c_ctx, w_ada, b_ada, norm_g, w_ffn_in, w_ffn_out, w_in_even, w_out_even,
           mlstm_conv, mlstm_gate_b, mlstm_head_g, diff_lambda, diff_subln_g, w_in_odd, w_out_odd, na_rpb):
    assert x.shape == (BATCH, SEQ, D_MODEL) and ctx.shape == (BATCH, CTX_LEN, D_MODEL)
    h = jnp.concatenate([x.reshape(T_LAT, D_MODEL), ctx.reshape(T_CTX, D_MODEL)], axis=0)
    c8 = jnp.concatenate([c, c_ctx[None], jnp.zeros((8 - BATCH - 1, D_MODEL), F32)], axis=0)
    mods_all = _ada_mods(c8, w_ada, b_ada)
    for l in range(DEPTH):
        last = l == DEPTH - 1
        mods = mods_all[l]
        g = norm_g[l]
        n_tiles = N_LAT_TILES if last else N_ALL_TILES
        act = _ffn_in(h, mods, g[0], w_ffn_in[l, 0].astype(BF16), 0, N_ALL_TILES)
        h = _out_proj(act, w_ffn_out[l, 0].astype(BF16), h, mods, g[1], 2, FFN_RES, N_ALL_TILES, FFN_TK)
        if l % 2 == 0:
            e = l // 2
            mix = _even_layer_mix(h, mods, g[2], w_in_even[e], mlstm_conv[e], mlstm_gate_b[e],
                                  mlstm_head_g[e], diff_lambda[e], diff_subln_g[e], l)
            w_out = w_out_even[e]
        else:
            assert last, "odd layers emit latent outputs only"
            o = l // 2
            mix = _odd_layer_mix(h, mods, g[2], w_in_odd[o], na_rpb[o])
            w_out = w_out_odd[o]
        h = _out_proj(mix, w_out.astype(BF16), h, mods, g[3], 5, 1.0, n_tiles, 1024)
        act = _ffn_in(h, mods, g[4], w_ffn_in[l, 1].astype(BF16), 6, n_tiles)
        h = _out_proj(act, w_ffn_out[l, 1].astype(BF16), h, mods, g[5], 8, FFN_RES, n_tiles, FFN_TK)
    return h[:T_LAT].reshape(BATCH, SEQ, D_MODEL)
```
